```python
import math
import jax, jax.numpy as jnp
from jax import lax
import numpy as np

D_MODEL = 2048
BATCH = 2
SEQ = 4096
DEPTH = 1

D_MIX = D_MODEL
D_NSA = D_MIX // 2
D_CONV = D_MIX - D_NSA
HEAD_DIM = 64
N_HEADS = D_NSA // HEAD_DIM
N_KV = 4
HPG = N_HEADS // N_KV
CMP_LEN = 32
CMP_STRIDE = 16
CMP_HIDDEN = 256
SLC_LEN = 64
N_SLC_SEL = 16
WINDOW = 512
Q_BLOCK = 128
FORCE_BONUS = 1.0e4
CONV_WIDTH = 31
PEER_HEADS = 8
N_KEYS = 128
N_EXPERTS = N_KEYS * N_KEYS
PK_DIM = 256
PK_HALF = PK_DIM // 2
PK_TOPK = 16
PEER_BLOCK = 128
Q_COLS = N_HEADS * HEAD_DIM
KV_COLS = 6 * N_KV * HEAD_DIM
GATE_COLS = 3 * N_HEADS
CONV_COLS = 2 * D_CONV
D_IN = Q_COLS + KV_COLS + GATE_COLS + CONV_COLS
LN_EPS = 1e-5
DN_ALPHA = (2 * DEPTH) ** 0.25
DN_BETA = (8 * DEPTH) ** -0.25

kernel_name = "hybrid_nsa_conformer_peer_block"


def layer_norm(x, g, b):
    xf = x.astype(jnp.float32)
    mu = jnp.mean(xf, axis=-1, keepdims=True)
    var = jnp.mean(jnp.square(xf - mu), axis=-1, keepdims=True)
    return ((xf - mu) * lax.rsqrt(var + LN_EPS) * g + b).astype(x.dtype)


def masked_softmax(s, mask):
    s = jnp.where(mask, s.astype(jnp.float32), -1e30)
    p = jax.nn.softmax(s, axis=-1)
    return jnp.where(mask, p, 0.0)


def alibi_slopes():
    sl = 2.0 ** (-8.0 * np.arange(1, N_HEADS + 1) / N_HEADS)
    return jnp.asarray(sl, jnp.float32).reshape(N_KV, HPG)


def compress_kv(kv, pos, w1, w2):
    bsz, seq = kv.shape[0], kv.shape[1]
    n_cmp = (seq - CMP_LEN) // CMP_STRIDE + 1
    idx = np.arange(n_cmp)[:, None] * CMP_STRIDE + np.arange(CMP_LEN)[None, :]
    blk = kv[:, idx] + pos[:, None, :]
    blk = jnp.transpose(blk, (0, 1, 3, 2, 4)).reshape(bsz, n_cmp, N_KV, CMP_LEN * HEAD_DIM)
    return jax.nn.gelu(blk @ w1) @ w2


def nsa_group(q, k_c, v_c, k_s, v_s, k_w, v_w, gates, pos_k, w1k, w2k, pos_v, w1v, w2v):
    bsz, seq = q.shape[0], q.shape[1]
    scale = HEAD_DIM ** -0.5
    slopes = alibi_slopes()
    kc = compress_kv(k_c, pos_k, w1k, w2k)
    vc = compress_kv(v_c, pos_v, w1v, w2v)
    n_cmp = kc.shape[1]
    n_slc = seq // SLC_LEN
    n_sel = min(N_SLC_SEL, n_slc)
    cmp_start = np.arange(n_cmp) * CMP_STRIDE
    cmp_end = cmp_start + CMP_LEN - 1
    slc_start = np.arange(n_slc) * SLC_LEN
    overlap = ((cmp_start[:, None] < slc_start[None, :] + SLC_LEN) &
               (cmp_start[:, None] + CMP_LEN > slc_start[None, :])).astype(np.float32)
    ks_blk = jnp.transpose(k_s.reshape(bsz, n_slc, SLC_LEN, N_KV, HEAD_DIM), (0, 3, 1, 2, 4))
    vs_blk = jnp.transpose(v_s.reshape(bsz, n_slc, SLC_LEN, N_KV, HEAD_DIM), (0, 3, 1, 2, 4))
    kw_pad = jnp.pad(k_w, ((0, 0), (WINDOW, 0), (0, 0), (0, 0)))
    vw_pad = jnp.pad(v_w, ((0, 0), (WINDOW, 0), (0, 0), (0, 0)))
    b_ix = jnp.arange(bsz)[:, None, None, None]
    g_ix = jnp.arange(N_KV)[None, None, :, None]
    jb = jnp.arange(n_slc)

    def block(j):
        t0 = j * Q_BLOCK
        qb = lax.dynamic_slice_in_dim(q, t0, Q_BLOCK, axis=1)
        gb = lax.dynamic_slice_in_dim(gates, t0, Q_BLOCK, axis=1).reshape(bsz, Q_BLOCK, N_KV, HPG, 3)
        tpos = t0 + jnp.arange(Q_BLOCK)
        dist_c = (tpos[:, None] - cmp_end[None, :]).astype(jnp.float32)
        s = jnp.einsum('bqghd,bngd->bghqn', qb, kc) * scale
        s = s - slopes[None, :, :, None, None] * dist_c
        p_c = masked_softmax(s, cmp_end[None, :] <= tpos[:, None])
        o_cmp = jnp.einsum('bghqn,bngd->bqghd', p_c, vc)
        imp = jnp.einsum('bghqn,nj->bqgj', p_c, overlap)
        cur = tpos // SLC_LEN
        forced = (jb[None, :] == 0) | (jb[None, :] == cur[:, None]) | (jb[None, :] == cur[:, None] - 1)
        imp = jnp.where(forced[None, :, None, :], imp + FORCE_BONUS, imp)
        imp = jnp.where((slc_start[None, :] <= tpos[:, None])[None, :, None, :], imp, -1.0)
        _, sel = lax.top_k(imp, n_sel)
        kg = ks_blk[b_ix, g_ix, sel]
        vg = vs_blk[b_ix, g_ix, sel]
        spos = sel[..., None] * SLC_LEN + jnp.arange(SLC_LEN)
        dist_s = (tpos[None, :, None, None, None] - spos).astype(jnp.float32)
        s = jnp.einsum('bqghd,bqgkld->bqghkl', qb, kg) * scale
        s = s - slopes[None, None, :, :, None, None] * dist_s[:, :, :, None]
        valid = jnp.broadcast_to((dist_s >= 0)[:, :, :, None], s.shape)
        p_s = masked_softmax(s.reshape(s.shape[:4] + (-1,)), valid.reshape(s.shape[:4] + (-1,)))
        o_slc = jnp.einsum('bqghkl,bqgkld->bqghd', p_s.reshape(s.shape), vg)
        kwb = lax.dynamic_slice_in_dim(kw_pad, t0, WINDOW + Q_BLOCK, axis=1)
        vwb = lax.dynamic_slice_in_dim(vw_pad, t0, WINDOW + Q_BLOCK, axis=1)
        wpos = t0 - WINDOW + jnp.arange(WINDOW + Q_BLOCK)
        dw = tpos[:, None] - wpos[None, :]
        mask_w = (dw >= 0) & (dw < WINDOW) & (wpos[None, :] >= 0)
        s = jnp.einsum('bqghd,bsgd->bghqs', qb, kwb) * scale
        s = s - slopes[None, :, :, None, None] * dw.astype(jnp.float32)
        p_w = masked_softmax(s, mask_w)
        o_win = jnp.einsum('bghqs,bsgd->bqghd', p_w, vwb)
        return gb[..., 0:1] * o_cmp + gb[..., 1:2] * o_slc + gb[..., 2:3] * o_win

    out = lax.map(block, jnp.arange(seq // Q_BLOCK))
    return jnp.transpose(out, (1, 0, 2, 3, 4, 5)).reshape(bsz, seq, D_NSA)


def conformer_conv_group(ab, dw_w, dw_b, ln_g, ln_b):
    a, gate = jnp.split(ab, 2, axis=-1)
    u = a * jax.nn.sigmoid(gate)
    u = jnp.pad(u, ((0, 0), (CONV_WIDTH - 1, 0), (0, 0)))
    c = lax.conv_general_dilated(u, dw_w[:, None, :], window_strides=(1,), padding='VALID',
                                 dimension_numbers=('NWC', 'WIO', 'NWC'),
                                 feature_group_count=D_CONV) + dw_b
    return jax.nn.silu(layer_norm(c, ln_g, ln_b))


def peer_ffn(y, wq, sub_keys, u_tab, v_tab):
    bsz, seq = y.shape[0], y.shape[1]
    yb_all = jnp.transpose(y.reshape(bsz, seq // PEER_BLOCK, PEER_BLOCK, D_MODEL), (1, 0, 2, 3))

    def block(yb):
        q = (yb @ wq).reshape(bsz, PEER_BLOCK, PEER_HEADS, 2, PK_HALF)
        s1 = jnp.einsum('bthd,hkd->bthk', q[..., 0, :], sub_keys[:, 0])
        s2 = jnp.einsum('bthd,hkd->bthk', q[..., 1, :], sub_keys[:, 1])
        v1, i1 = lax.top_k(s1, PK_TOPK)
        v2, i2 = lax.top_k(s2, PK_TOPK)
        cand = (v1[..., :, None] + v2[..., None, :]).reshape(bsz, PEER_BLOCK, PEER_HEADS, PK_TOPK * PK_TOPK)
        cidx = (i1[..., :, None] * N_KEYS + i2[..., None, :]).reshape(cand.shape)
        top, pos = lax.top_k(cand, PK_TOPK)
        eidx = jnp.take_along_axis(cidx, pos, axis=-1)
        g = jax.nn.softmax(top.astype(jnp.float32), axis=-1)
        u = u_tab[eidx]
        v = v_tab[eidx]
        h = jax.nn.gelu(jnp.einsum('btd,bthkd->bthk', yb, u))
        return jnp.einsum('bthk,bthkd->btd', g * h, v).astype(y.dtype)

    out = lax.map(block, yb_all)
    return jnp.transpose(out, (1, 0, 2, 3)).reshape(bsz, seq, D_MODEL)


def setup_inputs(seed: int = 0) -> dict:
    key = jax.random.key(seed)
    ks = jax.random.split(key, 24)
    f32 = jnp.float32
    L = DEPTH
    def nrm(k, shape, scale):
        return jax.random.normal(k, shape, f32) * scale
    return {
        "x": nrm(ks[0], (BATCH, SEQ, D_MODEL), 1.0),
        "w_in": nrm(ks[1], (L, D_MODEL, D_IN), D_MODEL ** -0.5),
        "cmp_pos_k": nrm(ks[2], (L, CMP_LEN, HEAD_DIM), 0.1),
        "cmp_w1_k": nrm(ks[3], (L, CMP_LEN * HEAD_DIM, CMP_HIDDEN), (CMP_LEN * HEAD_DIM) ** -0.5),
        "cmp_w2_k": nrm(ks[4], (L, CMP_HIDDEN, HEAD_DIM), CMP_HIDDEN ** -0.5),
        "cmp_pos_v": nrm(ks[5], (L, CMP_LEN, HEAD_DIM), 0.1),
        "cmp_w1_v": nrm(ks[6], (L, CMP_LEN * HEAD_DIM, CMP_HIDDEN), (CMP_LEN * HEAD_DIM) ** -0.5),
        "cmp_w2_v": nrm(ks[7], (L, CMP_HIDDEN, HEAD_DIM), CMP_HIDDEN ** -0.5),
        "dw_w": nrm(ks[8], (L, CONV_WIDTH, D_CONV), CONV_WIDTH ** -0.5),
        "dw_b": nrm(ks[9], (L, D_CONV), 0.02),
        "conv_ln_g": 1.0 + nrm(ks[10], (L, D_CONV), 0.02),
        "conv_ln_b": nrm(ks[11], (L, D_CONV), 0.02),
        "w_out": nrm(ks[12], (L, D_MIX, D_MODEL), D_MIX ** -0.5 * DN_BETA),
        "ln1_g": 1.0 + nrm(ks[13], (L, D_MODEL), 0.02),
        "ln1_b": nrm(ks[14], (L, D_MODEL), 0.02),
        "peer_wq": nrm(ks[15], (L, D_MODEL, PEER_HEADS * PK_DIM), D_MODEL ** -0.5),
        "peer_keys": nrm(ks[16], (L, PEER_HEADS, 2, N_KEYS, PK_HALF), PK_HALF ** -0.5),
        "peer_u": nrm(ks[17], (L, N_EXPERTS, D_MODEL), D_MODEL ** -0.5),
        "peer_v": nrm(ks[18], (L, N_EXPERTS, D_MODEL), DN_BETA * PEER_HEADS ** -0.5),
        "ln2_g": 1.0 + nrm(ks[19], (L, D_MODEL), 0.02),
        "ln2_b": nrm(ks[20], (L, D_MODEL), 0.02),
    }


def reference(x, w_in, cmp_pos_k, cmp_w1_k, cmp_w2_k, cmp_pos_v, cmp_w1_v, cmp_w2_v,
              dw_w, dw_b, conv_ln_g, conv_ln_b, w_out, ln1_g, ln1_b,
              peer_wq, peer_keys, peer_u, peer_v, ln2_g, ln2_b):
    bsz, seq = x.shape[0], x.shape[1]
    splits = [Q_COLS, Q_COLS + KV_COLS, Q_COLS + KV_COLS + GATE_COLS]
    for l in range(DEPTH):
        h = x @ w_in[l]
        q, kv, gl, conv_ab = jnp.split(h, splits, axis=-1)
        q = q.reshape(bsz, seq, N_KV, HPG, HEAD_DIM)
        kv = kv.reshape(bsz, seq, 6, N_KV, HEAD_DIM)
        gates = jax.nn.sigmoid(gl).reshape(bsz, seq, N_HEADS, 3)
        o_nsa = nsa_group(q, kv[:, :, 0], kv[:, :, 1], kv[:, :, 2], kv[:, :, 3], kv[:, :, 4], kv[:, :, 5],
                          gates, cmp_pos_k[l], cmp_w1_k[l], cmp_w2_k[l],
                          cmp_pos_v[l], cmp_w1_v[l], cmp_w2_v[l]).astype(x.dtype)
        o_conv = conformer_conv_group(conv_ab, dw_w[l], dw_b[l], conv_ln_g[l], conv_ln_b[l]).astype(x.dtype)
        mix = jnp.concatenate([o_nsa, o_conv], axis=-1) @ w_out[l]
        x = layer_norm(DN_ALPHA * x + mix, ln1_g[l], ln1_b[l])
        ffn = peer_ffn(x, peer_wq[l], peer_keys[l], peer_u[l], peer_v[l])
        x = layer_norm(DN_ALPHA * x + ffn, ln2_g[l], ln2_b[l])
    return x
```

```python
import functools

import numpy as np
import jax
import jax.numpy as jnp
from jax import lax
from jax.experimental import pallas as pl
from jax.experimental.pallas import tpu as pltpu

F32 = jnp.float32
BF16 = jnp.bfloat16

D_MODEL = 2048
HEAD_DIM = 64
N_HEADS = 16
N_KV = 4
HPG = N_HEADS // N_KV
CMP_LEN = 32
CMP_STRIDE = 16
CMP_HIDDEN = 256
SLC_LEN = 64
N_SLC_SEL = 16
WINDOW = 512
Q_BLOCK = 128
FORCE_BONUS = 1.0e4
CONV_WIDTH = 31
D_CONV = 1024
PEER_HEADS = 8
N_KEYS = 128
PK_TOPK = 16
Q_COLS = N_HEADS * HEAD_DIM
KV_COLS = 6 * N_KV * HEAD_DIM
GATE_COLS = 3 * N_HEADS
LN_EPS = 1e-5
DEPTH = 1
DN_ALPHA = (2 * DEPTH) ** 0.25
NEG = -1e30

VMEM_LIMIT = 56 * 1024 * 1024
LANES = 128
SLC_KEY_TILE = 512
N_CHUNK = 256
HEADS_PER_PAIR = 2 * HPG


def _cparams(sem):
    return pltpu.CompilerParams(dimension_semantics=sem, vmem_limit_bytes=VMEM_LIMIT)


def _gelu_tanh(x):
    c = np.sqrt(2.0 / np.pi).astype(np.float32)
    return 0.5 * x * (1.0 + jnp.tanh(c * (x + 0.044715 * (x * x * x))))


def _layer_norm_rows(z, g, b):
    mu = jnp.mean(z, axis=-1, keepdims=True)
    zc = z - mu
    var = jnp.mean(zc * zc, axis=-1, keepdims=True)
    return zc * lax.rsqrt(var + LN_EPS) * g + b


def _dot_nt(a, b):
    return lax.dot_general(a, b, (((1,), (1,)), ((), ())), preferred_element_type=F32)


def _mm_kernel(x_ref, w_ref, o_ref, *, act):
    acc = jnp.dot(x_ref[...], w_ref[...], preferred_element_type=F32)
    if act == "sigmoid":
        acc = jax.nn.sigmoid(acc)
    o_ref[...] = acc.astype(o_ref.dtype)


def _matmul(x, w, out_dtype, tm, tn, act=None):
    m, k = x.shape
    n = w.shape[1]
    return pl.pallas_call(
        functools.partial(_mm_kernel, act=act),
        grid=(m // tm, n // tn),
        in_specs=[pl.BlockSpec((tm, k), lambda i, j: (i, 0)),
                  pl.BlockSpec((k, tn), lambda i, j: (0, j))],
        out_specs=pl.BlockSpec((tm, tn), lambda i, j: (i, j)),
        out_shape=jax.ShapeDtypeStruct((m, n), out_dtype),
        compiler_params=_cparams(("parallel", "parallel")),
        name="proj_matmul",
    )(x, w)


def _compress_kernel(ch_ref, pos_ref, w1_ref, w2_ref, o_ref):
    rows = lax.broadcasted_iota(jnp.int32, (N_CHUNK, 1), 0)
    for pair in range(2):
        pieces = []
        for kv in range(2):
            for hf in range(2):
                g = 2 * pair + hf
                c = ch_ref[kv * N_KV + g].astype(F32)
                a_in = (c + pos_ref[kv, 0]).astype(BF16)
                b_in = (c + pos_ref[kv, 1]).astype(BF16)
                a = jnp.dot(a_in, w1_ref[kv, 0], preferred_element_type=F32)
                b = jnp.dot(b_in, w1_ref[kv, 1], preferred_element_type=F32)
                pre = a + pltpu.roll(b, N_CHUNK - 1, 0)
                hid = _gelu_tanh(pre).astype(BF16)
                out = jnp.dot(hid, w2_ref[kv], preferred_element_type=F32)
                out = jnp.where(rows < N_CHUNK - 1, out, 0.0)
                pieces.append(out)
        o_ref[pair] = jnp.concatenate(pieces, axis=1).astype(o_ref.dtype)


def _compress(chunks, pos, w1, w2):
    bsz = chunks.shape[0]
    return pl.pallas_call(
        _compress_kernel,
        grid=(bsz,),
        in_specs=[pl.BlockSpec((None, 8, N_CHUNK, 1024), lambda b: (b, 0, 0, 0)),
                  pl.BlockSpec((2, 2, 1, 1024), lambda b: (0, 0, 0, 0)),
                  pl.BlockSpec((2, 2, 1024, CMP_HIDDEN), lambda b: (0, 0, 0, 0)),
                  pl.BlockSpec((2, CMP_HIDDEN, HEAD_DIM), lambda b: (0, 0, 0))],
        out_specs=pl.BlockSpec((None, 2, N_CHUNK, 256), lambda b: (b, 0, 0, 0)),
        out_shape=jax.ShapeDtypeStruct((bsz, 2, N_CHUNK, 256), BF16),
        compiler_params=_cparams(("parallel",)),
        name="nsa_compress",
    )(chunks, pos, w1, w2)


def _softmax_rows(s, valid):
    m = jnp.max(s, axis=1, keepdims=True)
    p = jnp.where(valid, jnp.exp(s - m), 0.0)
    l = jnp.sum(p, axis=1, keepdims=True)
    return p * jnp.where(l > 0.0, 1.0 / l, 0.0)


def _attn_kernel(slopes_ref, q_ref, kcv_ref, ks_ref, vs_ref, kw_ref, vw_ref, g_ref, ov_ref,
                 o_ref, m_ref, l_ref, acc_ref):
    gp = pl.program_id(1)
    qb = pl.program_id(2)
    t0 = qb * Q_BLOCK
    n_q4 = HPG * Q_BLOCK
    tq = t0 + lax.broadcasted_iota(jnp.int32, (Q_BLOCK, 1), 0)
    tq4 = jnp.concatenate([tq] * HPG, axis=0)
    qblk = q_ref[...]
    gates = g_ref[...]
    zeros64 = jnp.zeros((Q_BLOCK, HEAD_DIM), BF16)
    outs = []
    for hf in range(2):
        g = 2 * gp + hf
        parts = []
        slope_parts = []
        for hg in range(HPG):
            h = hf * HPG + hg
            qh = (qblk[:, h * HEAD_DIM:(h + 1) * HEAD_DIM].astype(F32) * (HEAD_DIM ** -0.5)).astype(BF16)
            parts.append(jnp.concatenate([qh, zeros64] if hf == 0 else [zeros64, qh], axis=1))
            slope_parts.append(jnp.full((Q_BLOCK, 1), slopes_ref[g * HPG + hg], F32))
        q4 = jnp.concatenate(parts, axis=0)
        slope = jnp.concatenate(slope_parts, axis=0)

        kcp = kcv_ref[:, 0:LANES]
        vcp = kcv_ref[:, LANES:2 * LANES]
        cend = lax.broadcasted_iota(jnp.int32, (1, N_CHUNK), 1) * CMP_STRIDE + (CMP_LEN - 1)
        valid_c = cend <= tq4
        s = _dot_nt(q4, kcp) - slope * (tq4 - cend).astype(F32)
        p_c = _softmax_rows(jnp.where(valid_c, s, NEG), valid_c)
        o_cmp = jnp.dot(p_c.astype(BF16), vcp, preferred_element_type=F32)

        psum = p_c[0:Q_BLOCK]
        for hg in range(1, HPG):
            psum = psum + p_c[hg * Q_BLOCK:(hg + 1) * Q_BLOCK]
        imp = jnp.dot(psum, ov_ref[...], preferred_element_type=F32,
                      precision=lax.Precision.HIGHEST)
        n_slc = imp.shape[1]
        jb = lax.broadcasted_iota(jnp.int32, (Q_BLOCK, n_slc), 1)
        cur = tq >> 6
        forced = (jb == 0) | (jb == cur) | (jb == cur - 1)
        imp = jnp.where(forced, imp + FORCE_BONUS, imp)
        imp = jnp.where(jb <= cur, imp, -1.0)
        rank = jnp.zeros((Q_BLOCK, n_slc), F32)
        for i in range(n_slc):
            ci = imp[:, i:i + 1]
            ge = jnp.where(ci >= imp, 1.0, 0.0)
            gt = jnp.where(ci > imp, 1.0, 0.0)
            rank = rank + jnp.where(jb > i, ge, gt)
        sel = jnp.where(rank < float(N_SLC_SEL), 1.0, 0.0).astype(BF16)

        m_ref[...] = jnp.full((n_q4, LANES), NEG, F32)
        l_ref[...] = jnp.zeros((n_q4, LANES), F32)
        acc_ref[...] = jnp.zeros((n_q4, LANES), F32)
        blocks_per_tile = SLC_KEY_TILE // SLC_LEN

        def slc_body(kt, carry):
            k0 = pl.multiple_of(kt * SLC_KEY_TILE, SLC_KEY_TILE)
            k_tile = ks_ref[pl.ds(k0, SLC_KEY_TILE), :]
            v_tile = vs_ref[pl.ds(k0, SLC_KEY_TILE), :]
            kpos = k0 + lax.broadcasted_iota(jnp.int32, (1, SLC_KEY_TILE), 1)
            jj = lax.broadcasted_iota(jnp.int32, (n_slc, SLC_KEY_TILE), 0)
            kk = lax.broadcasted_iota(jnp.int32, (n_slc, SLC_KEY_TILE), 1)
            expand = jnp.where((kk >> 6) + kt * blocks_per_tile == jj, 1.0, 0.0).astype(BF16)
            mexp = jnp.dot(sel, expand, preferred_element_type=F32)
            vf = jnp.where(kpos <= tq, mexp, 0.0)
            valid = jnp.concatenate([vf] * HPG, axis=0) > 0.5
            s_t = _dot_nt(q4, k_tile) + slope * (kpos - tq4).astype(F32)
            s_t = jnp.where(valid, s_t, NEG)
            m_prev = m_ref[...]
            m_cur = jnp.max(s_t, axis=1, keepdims=True)
            m_next = jnp.maximum(m_prev, m_cur)
            p = jnp.where(valid, jnp.exp(s_t - m_next[:, 0:1]), 0.0)
            alpha = jnp.exp(m_prev - m_next)
            l_ref[...] = alpha * l_ref[...] + jnp.sum(p, axis=1, keepdims=True)
            acc_ref[...] = alpha * acc_ref[...] + jnp.dot(p.astype(BF16), v_tile,
                                                          preferred_element_type=F32)
            m_ref[...] = m_next
            return carry

        lax.fori_loop(0, qb // (SLC_KEY_TILE // Q_BLOCK) + 1, slc_body, 0)
        l_s = l_ref[...]
        o_slc = acc_ref[...] * jnp.where(l_s > 0.0, 1.0 / l_s, 0.0)

        n_win = WINDOW + Q_BLOCK
        w0 = pl.multiple_of(jnp.maximum(t0 - WINDOW, 0), Q_BLOCK)
        kw_tile = kw_ref[pl.ds(w0, n_win), :]
        vw_tile = vw_ref[pl.ds(w0, n_win), :]
        dw = tq4 - (w0 + lax.broadcasted_iota(jnp.int32, (1, n_win), 1))
        valid_w = (dw >= 0) & (dw < WINDOW)
        s_w = _dot_nt(q4, kw_tile) - slope * dw.astype(F32)
        p_w = _softmax_rows(jnp.where(valid_w, s_w, NEG), valid_w)
        o_win = jnp.dot(p_w.astype(BF16), vw_tile, preferred_element_type=F32)

        for hg in range(HPG):
            h = hf * HPG + hg
            rs = slice(hg * Q_BLOCK, (hg + 1) * Q_BLOCK)
            cs = slice(hf * HEAD_DIM, (hf + 1) * HEAD_DIM)
            outs.append(gates[:, 3 * h:3 * h + 1] * o_cmp[rs, cs]
                        + gates[:, 3 * h + 1:3 * h + 2] * o_slc[rs, cs]
                        + gates[:, 3 * h + 2:3 * h + 3] * o_win[rs, cs])
    o_ref[...] = jnp.concatenate(outs, axis=1).astype(o_ref.dtype)


def _nsa_attention(slopes, qkv, kcv, gates, overlap):
    bsz, seq, _ = qkv.shape
    pair_w = HEADS_PER_PAIR * HEAD_DIM
    kv0 = Q_COLS // LANES
    blk = N_KV * HEAD_DIM // LANES

    def kv_spec(idx):
        return pl.BlockSpec((None, seq, LANES), lambda b, gp, qb: (b, 0, kv0 + idx * blk + gp))

    n_q4 = HPG * Q_BLOCK
    return pl.pallas_call(
        _attn_kernel,
        grid=(bsz, 2, seq // Q_BLOCK),
        in_specs=[pl.BlockSpec(memory_space=pltpu.SMEM),
                  pl.BlockSpec((None, Q_BLOCK, pair_w), lambda b, gp, qb: (b, qb, gp)),
                  pl.BlockSpec((None, None, N_CHUNK, 256), lambda b, gp, qb: (b, gp, 0, 0)),
                  kv_spec(2), kv_spec(3), kv_spec(4), kv_spec(5),
                  pl.BlockSpec((None, Q_BLOCK, LANES), lambda b, gp, qb: (b, qb, gp)),
                  pl.BlockSpec((N_CHUNK, seq // SLC_LEN), lambda b, gp, qb: (0, 0))],
        out_specs=pl.BlockSpec((None, Q_BLOCK, pair_w), lambda b, gp, qb: (b, qb, gp)),
        out_shape=jax.ShapeDtypeStruct((bsz, seq, Q_COLS), BF16),
        scratch_shapes=[pltpu.VMEM((n_q4, LANES), F32)] * 3,
        compiler_params=_cparams(("parallel", "parallel", "arbitrary")),
        name="nsa_attention",
    )(slopes, qkv, kcv, qkv, qkv, qkv, qkv, gates, overlap)


CONV_TILE = 256
CONV_HALO = 32
CONV_CHUNK = 32


def _conv_kernel(ab_ref, halo_ref, w_ref, b_ref, g_ref, beta_ref, o_ref, u_ref):
    i = pl.program_id(1)
    ab = ab_ref[...]
    u_ref[CONV_HALO:, :] = ab[:, :D_CONV] * jax.nn.sigmoid(ab[:, D_CONV:])
    hb = halo_ref[...]
    uh = hb[:, :D_CONV] * jax.nn.sigmoid(hb[:, D_CONV:])
    u_ref[0:CONV_HALO, :] = jnp.where(i > 0, uh, 0.0)
    w = w_ref[...]
    off = CONV_HALO - (CONV_WIDTH - 1)

    for c in range(CONV_TILE // CONV_CHUNK):
        r0 = c * CONV_CHUNK
        acc = jnp.zeros((CONV_CHUNK, D_CONV), F32) + b_ref[...]
        for k in range(CONV_WIDTH):
            acc = acc + u_ref[r0 + off + k:r0 + off + k + CONV_CHUNK, :] * w[k:k + 1, :]
        y = _layer_norm_rows(acc, g_ref[...], beta_ref[...])
        o_ref[r0:r0 + CONV_CHUNK, :] = (y * jax.nn.sigmoid(y)).astype(o_ref.dtype)


def _conv_group(conv_ab, dw_w, dw_b, ln_g, ln_b):
    bsz, seq, width = conv_ab.shape
    ratio = CONV_TILE // CONV_HALO
    return pl.pallas_call(
        _conv_kernel,
        grid=(bsz, seq // CONV_TILE),
        in_specs=[pl.BlockSpec((None, CONV_TILE, width), lambda b, i: (b, i, 0)),
                  pl.BlockSpec((None, CONV_HALO, width), lambda b, i: (b, jnp.maximum(i * ratio - 1, 0), 0)),
                  pl.BlockSpec((CONV_WIDTH, D_CONV), lambda b, i: (0, 0)),
                  pl.BlockSpec((1, D_CONV), lambda b, i: (0, 0)),
                  pl.BlockSpec((1, D_CONV), lambda b, i: (0, 0)),
                  pl.BlockSpec((1, D_CONV), lambda b, i: (0, 0))],
        out_specs=pl.BlockSpec((None, CONV_TILE, D_CONV), lambda b, i: (b, i, 0)),
        out_shape=jax.ShapeDtypeStruct((bsz, seq, D_CONV), BF16),
        scratch_shapes=[pltpu.VMEM((CONV_HALO + CONV_TILE, D_CONV), F32)],
        compiler_params=_cparams(("parallel", "parallel")),
        name="conformer_conv",
    )(conv_ab, conv_ab, dw_w, dw_b, ln_g, ln_b)


def _outproj_kernel(on_ref, oc_ref, x_ref, w_ref, g_ref, b_ref, y_ref, yt_ref):
    k_nsa = on_ref.shape[1]
    mix = jnp.dot(on_ref[...], w_ref[0:k_nsa, :], preferred_element_type=F32)
    mix = mix + jnp.dot(oc_ref[...], w_ref[k_nsa:, :], preferred_element_type=F32)
    y = _layer_norm_rows(DN_ALPHA * x_ref[...] + mix, g_ref[...], b_ref[...])
    y_ref[...] = y
    yt_ref[...] = y.T.astype(yt_ref.dtype)


def _outproj_ln(o_nsa, o_conv, x, w_out, g, b, tm=256):
    m, d = x.shape
    return pl.pallas_call(
        _outproj_kernel,
        grid=(m // tm,),
        in_specs=[pl.BlockSpec((tm, o_nsa.shape[1]), lambda i: (i, 0)),
                  pl.BlockSpec((tm, o_conv.shape[1]), lambda i: (i, 0)),
                  pl.BlockSpec((tm, d), lambda i: (i, 0)),
                  pl.BlockSpec(w_out.shape, lambda i: (0, 0)),
                  pl.BlockSpec((1, d), lambda i: (0, 0)),
                  pl.BlockSpec((1, d), lambda i: (0, 0))],
        out_specs=[pl.BlockSpec((tm, d), lambda i: (i, 0)),
                   pl.BlockSpec((d, tm), lambda i: (0, i))],
        out_shape=[jax.ShapeDtypeStruct((m, d), F32),
                   jax.ShapeDtypeStruct((d, m), BF16)],
        compiler_params=_cparams(("parallel",)),
        name="outproj_ln1",
    )(o_nsa, o_conv, x, w_out, g, b)


def _sort_pairs(n):
    pairs = []
    p = 1
    while p < n:
        k = p
        while k >= 1:
            for j in range(k % p, n - k, 2 * k):
                for i in range(min(k, n - j - k)):
                    if (i + j) // (2 * p) == (i + j + k) // (2 * p):
                        pairs.append((i + j, i + j + k))
            k //= 2
        p *= 2
    return pairs


_SORT16 = _sort_pairs(PK_TOPK)
_CAND = [(i, j) for i in range(PK_TOPK) for j in range(PK_TOPK) if (i + 1) * (j + 1) <= PK_TOPK]


def _sort_desc(xs):
    xs = list(xs)
    for (i, j) in _SORT16:
        hi = jnp.maximum(xs[i], xs[j])
        lo = jnp.minimum(xs[i], xs[j])
        xs[i], xs[j] = hi, lo
    return xs


def _merge_top(xs, ys):
    n = len(xs)
    zs = [jnp.maximum(xs[i], ys[n - 1 - i]) for i in range(n)]
    k = n // 2
    while k >= 1:
        for i in range(n):
            if (i & k) == 0:
                hi = jnp.maximum(zs[i], zs[i + k])
                lo = jnp.minimum(zs[i], zs[i + k])
                zs[i], zs[i + k] = hi, lo
        k //= 2
    return zs


def _peer_score_kernel(yt_ref, wqt_ref, keys_ref, s1_ref, e1_ref, s2_ref, e2_ref, tau_ref):
    tm = yt_ref.shape[1]
    qt = jnp.dot(wqt_ref[...], yt_ref[...], preferred_element_type=F32)
    sub = lax.broadcasted_iota(jnp.int32, (8, tm), 0)
    tops = [[None] * PK_TOPK, [None] * PK_TOPK]
    for hh in range(2 * PEER_HEADS):
        h, half = hh // 2, hh % 2
        qh = qt[hh * N_KEYS:(hh + 1) * N_KEYS, :].astype(BF16)
        st = jnp.dot(keys_ref[hh], qh, preferred_element_type=F32)
        (s1_ref if half == 0 else s2_ref)[h] = st
        xs = _sort_desc([st[8 * r:8 * r + 8, :] for r in range(N_KEYS // 8)])
        for sh in (4, 2, 1):
            xs = _merge_top(xs, [pltpu.roll(x, sh, 0) for x in xs])
        for i in range(PK_TOPK):
            tops[half][i] = xs[i] if h == 0 else jnp.where(sub == h, xs[i], tops[half][i])
    v1, v2 = tops
    cands = [v1[i] + v2[j] for (i, j) in _CAND]
    c16 = None
    for ca in cands:
        cnt = jnp.zeros_like(ca)
        for cb in cands:
            cnt = cnt + jnp.where(cb >= ca, 1.0, 0.0)
        cand_ok = jnp.where(cnt >= float(PK_TOPK), ca, -jnp.inf)
        c16 = cand_ok if c16 is None else jnp.maximum(c16, cand_ok)
    mx = cands[0]
    z = jnp.zeros_like(mx)
    for ca in cands:
        z = z + jnp.where(ca >= c16, jnp.exp(ca - mx), 0.0)
    zinv = 1.0 / z
    tau_ref[...] = c16
    for h in range(PEER_HEADS):
        e1_ref[h] = jnp.exp(s1_ref[h] - v1[0][h:h + 1, :])
        e2_ref[h] = jnp.exp(s2_ref[h] - v2[0][h:h + 1, :]) * zinv[h:h + 1, :]


def _peer_scores(yt, wqt, keys, tm=256):
    d, m = yt.shape
    big = jax.ShapeDtypeStruct((PEER_HEADS, N_KEYS, m), F32)
    big_spec = pl.BlockSpec((PEER_HEADS, N_KEYS, tm), lambda i: (0, 0, i))
    return pl.pallas_call(
        _peer_score_kernel,
        grid=(m // tm,),
        in_specs=[pl.BlockSpec((d, tm), lambda i: (0, i)),
                  pl.BlockSpec(wqt.shape, lambda i: (0, 0)),
                  pl.BlockSpec(keys.shape, lambda i: (0, 0, 0))],
        out_specs=[big_spec, big_spec, big_spec, big_spec,
                   pl.BlockSpec((PEER_HEADS, tm), lambda i: (0, i))],
        out_shape=[big, big, big, big, jax.ShapeDtypeStruct((PEER_HEADS, m), F32)],
        compiler_params=_cparams(("parallel",)),
        name="peer_scores",
    )(yt, wqt, keys)


PEER_TM = 512
PEER_TE = 512


def _peer_mix_kernel(yt_ref, u_ref, vt_ref, s1_ref, e1_ref, s2_ref, e2_ref, tau_ref, o_ref, g_ref):
    j = pl.program_id(1)

    @pl.when(j == 0)
    def _():
        o_ref[...] = jnp.zeros_like(o_ref)

    hid = jnp.dot(u_ref[...], yt_ref[...], preferred_element_type=F32)
    gh = _gelu_tanh(hid)
    n_a = PEER_TE // N_KEYS
    for ai in range(n_a):
        a = j * n_a + ai
        w = None
        for h in range(PEER_HEADS):
            s = s1_ref[h, pl.ds(a, 1), :] + s2_ref[h]
            contrib = jnp.where(s >= tau_ref[h:h + 1, :], e1_ref[h, pl.ds(a, 1), :] * e2_ref[h], 0.0)
            w = contrib if w is None else w + contrib
        g_ref[ai * N_KEYS:(ai + 1) * N_KEYS, :] = (w * gh[ai * N_KEYS:(ai + 1) * N_KEYS, :]).astype(BF16)
    o_ref[...] += jnp.dot(vt_ref[...], g_ref[...], preferred_element_type=F32)


def _peer_mix(yt, u, vt, s1, e1, s2, e2, tau):
    d, m = yt.shape
    n_e = u.shape[0]
    big_spec = pl.BlockSpec((PEER_HEADS, N_KEYS, PEER_TM), lambda i, j: (0, 0, i))
    return pl.pallas_call(
        _peer_mix_kernel,
        grid=(m // PEER_TM, n_e // PEER_TE),
        in_specs=[pl.BlockSpec((d, PEER_TM), lambda i, j: (0, i)),
                  pl.BlockSpec((PEER_TE, d), lambda i, j: (j, 0)),
                  pl.BlockSpec((d, PEER_TE), lambda i, j: (0, j)),
                  big_spec, big_spec, big_spec, big_spec,
                  pl.BlockSpec((PEER_HEADS, PEER_TM), lambda i, j: (0, i))],
        out_specs=pl.BlockSpec((d, PEER_TM), lambda i, j: (0, i)),
        out_shape=jax.ShapeDtypeStruct((d, m), F32),
        scratch_shapes=[pltpu.VMEM((PEER_TE, PEER_TM), BF16)],
        compiler_params=_cparams(("parallel", "arbitrary")),
        name="peer_mix",
    )(yt, u, vt, s1, e1, s2, e2, tau)


def _final_ln_kernel(ft_ref, y_ref, g_ref, b_ref, o_ref):
    z = DN_ALPHA * y_ref[...] + ft_ref[...].T
    o_ref[...] = _layer_norm_rows(z, g_ref[...], b_ref[...])


def _final_ln(ffn_t, y, g, b, tm=256):
    m, d = y.shape
    return pl.pallas_call(
        _final_ln_kernel,
        grid=(m // tm,),
        in_specs=[pl.BlockSpec((d, tm), lambda i: (0, i)),
                  pl.BlockSpec((tm, d), lambda i: (i, 0)),
                  pl.BlockSpec((1, d), lambda i: (0, 0)),
                  pl.BlockSpec((1, d), lambda i: (0, 0))],
        out_specs=pl.BlockSpec((tm, d), lambda i: (i, 0)),
        out_shape=jax.ShapeDtypeStruct((m, d), F32),
        compiler_params=_cparams(("parallel",)),
        name="final_ln2",
    )(ffn_t, y, g, b)


def _alibi_slopes():
    return jnp.asarray(2.0 ** (-8.0 * np.arange(1, N_HEADS + 1) / N_HEADS), F32)


def _overlap_matrix(seq):
    n_cmp = (seq - CMP_LEN) // CMP_STRIDE + 1
    n_slc = seq // SLC_LEN
    cmp_start = np.arange(N_CHUNK) * CMP_STRIDE
    slc_start = np.arange(n_slc) * SLC_LEN
    ov = ((cmp_start[:, None] < slc_start[None, :] + SLC_LEN)
          & (cmp_start[:, None] + CMP_LEN > slc_start[None, :])
          & (np.arange(N_CHUNK)[:, None] < n_cmp)).astype(np.float32)
    return jnp.asarray(ov)


def _layer(x, w_in, cmp_pos_k, cmp_w1_k, cmp_w2_k, cmp_pos_v, cmp_w1_v, cmp_w2_v,
           dw_w, dw_b, conv_ln_g, conv_ln_b, w_out, ln1_g, ln1_b,
           peer_wq, peer_keys, peer_u, peer_v, ln2_g, ln2_b):
    bsz, seq, d = x.shape
    m = bsz * seq
    assert seq // CMP_STRIDE == N_CHUNK and d == D_MODEL
    x2 = x.reshape(m, d)
    xb = x2.astype(BF16)

    n_qkv = Q_COLS + KV_COLS
    w_qkv = w_in[:, :n_qkv].astype(BF16)
    w_gl = w_in[:, n_qkv:n_qkv + GATE_COLS]
    per_pair = 3 * HEADS_PER_PAIR
    w_gate = jnp.pad(w_gl.reshape(d, 2, per_pair), ((0, 0), (0, 0), (0, LANES - per_pair)))
    w_gate = w_gate.reshape(d, 2 * LANES).astype(BF16)
    w_conv = w_in[:, n_qkv + GATE_COLS:].astype(BF16)
    qkv = _matmul(xb, w_qkv, BF16, 1024, 512).reshape(bsz, seq, n_qkv)
    gates = _matmul(xb, w_gate, F32, 1024, 2 * LANES, act="sigmoid").reshape(bsz, seq, 2 * LANES)
    conv_ab = _matmul(xb, w_conv, F32, 1024, 512).reshape(bsz, seq, 2 * D_CONV)

    cmp_cols = 2 * N_KV * HEAD_DIM
    chunks = qkv[:, :, Q_COLS:Q_COLS + cmp_cols].reshape(bsz, N_CHUNK, CMP_STRIDE, 2 * N_KV, HEAD_DIM)
    chunks = jnp.transpose(chunks, (0, 3, 1, 2, 4)).reshape(bsz, 2 * N_KV, N_CHUNK, CMP_STRIDE * HEAD_DIM)
    half = CMP_STRIDE * HEAD_DIM
    pos = jnp.stack([cmp_pos_k.reshape(2, 1, half), cmp_pos_v.reshape(2, 1, half)])
    w1 = jnp.stack([cmp_w1_k.reshape(2, half, CMP_HIDDEN), cmp_w1_v.reshape(2, half, CMP_HIDDEN)]).astype(BF16)
    w2 = jnp.stack([cmp_w2_k, cmp_w2_v]).astype(BF16)
    kcv = _compress(chunks, pos, w1, w2)
    o_nsa = _nsa_attention(_alibi_slopes(), qkv, kcv, gates, _overlap_matrix(seq))

    o_conv = _conv_group(conv_ab, dw_w, dw_b.reshape(1, -1), conv_ln_g.reshape(1, -1), conv_ln_b.reshape(1, -1))

    y, yt = _outproj_ln(o_nsa.reshape(m, Q_COLS), o_conv.reshape(m, D_CONV), x2, w_out.astype(BF16),
                        ln1_g.reshape(1, -1), ln1_b.reshape(1, -1))

    wqt = peer_wq.T.astype(BF16)
    keys = peer_keys.reshape(2 * PEER_HEADS, N_KEYS, -1).astype(BF16)
    s1, e1, s2, e2, tau = _peer_scores(yt, wqt, keys)
    ffn_t = _peer_mix(yt, peer_u.astype(BF16), peer_v.T.astype(BF16), s1, e1, s2, e2, tau)
    out = _final_ln(ffn_t, y, ln2_g.reshape(1, -1), ln2_b.reshape(1, -1))
    return out.reshape(bsz, seq, d)


def kernel(x, w_in, cmp_pos_k, cmp_w1_k, cmp_w2_k, cmp_pos_v, cmp_w1_v, cmp_w2_v, dw_w, dw_b,
           conv_ln_g, conv_ln_b, w_out, ln1_g, ln1_b, peer_wq, peer_keys, peer_u, peer_v, ln2_g, ln2_b):
    for l in range(DEPTH):
        x = _layer(x, w_in[l], cmp_pos_k[l], cmp_w1_k[l], cmp_w2_k[l], cmp_pos_v[l], cmp_w1_v[l],
                   cmp_w2_v[l], dw_w[l], dw_b[l], conv_ln_g[l], conv_ln_b[l], w_out[l], ln1_g[l],
                   ln1_b[l], peer_wq[l], peer_keys[l], peer_u[l], peer_v[l], ln2_g[l], ln2_b[l])
    return x
```

```python
import functools

import numpy as np
import jax
import jax.numpy as jnp
from jax import lax
from jax.experimental import pallas as pl
from jax.experimental.pallas import tpu as pltpu

F32 = jnp.float32
BF16 = jnp.bfloat16

D_MODEL = 2048
HEAD_DIM = 64
N_HEADS = 16
N_KV = 4
HPG = N_HEADS // N_KV
CMP_LEN = 32
CMP_STRIDE = 16
CMP_HIDDEN = 256
SLC_LEN = 64
N_SLC_SEL = 16
WINDOW = 512
Q_BLOCK = 128
FORCE_BONUS = 1.0e4
CONV_WIDTH = 31
D_CONV = 1024
PEER_HEADS = 8
N_KEYS = 128
PK_TOPK = 16
Q_COLS = N_HEADS * HEAD_DIM
KV_COLS = 6 * N_KV * HEAD_DIM
GATE_COLS = 3 * N_HEADS
LN_EPS = 1e-5
DEPTH = 1
DN_ALPHA = (2 * DEPTH) ** 0.25
NEG = -1e30

VMEM_LIMIT = 56 * 1024 * 1024
LANES = 128
SLC_KEY_TILE = 1024
N_CHUNK = 256
HEADS_PER_PAIR = 2 * HPG


def _cparams(sem):
    return pltpu.CompilerParams(dimension_semantics=sem, vmem_limit_bytes=VMEM_LIMIT)


def _gelu_tanh(x):
    c = np.sqrt(2.0 / np.pi).astype(np.float32)
    return 0.5 * x * (1.0 + jnp.tanh(c * (x + 0.044715 * (x * x * x))))


def _layer_norm_rows(z, g, b):
    mu = jnp.mean(z, axis=-1, keepdims=True)
    zc = z - mu
    var = jnp.mean(zc * zc, axis=-1, keepdims=True)
    return zc * lax.rsqrt(var + LN_EPS) * g + b


def _dot_nt(a, b):
    return lax.dot_general(a, b, (((1,), (1,)), ((), ())), preferred_element_type=F32)


def _mm_kernel(x_ref, w_ref, o_ref, *, act):
    acc = jnp.dot(x_ref[...], w_ref[...], preferred_element_type=F32)
    if act == "sigmoid":
        acc = jax.nn.sigmoid(acc)
    o_ref[...] = acc.astype(o_ref.dtype)


def _matmul(x, w, out_dtype, tm, tn, act=None):
    m, k = x.shape
    n = w.shape[1]
    return pl.pallas_call(
        functools.partial(_mm_kernel, act=act),
        grid=(m // tm, n // tn),
        in_specs=[pl.BlockSpec((tm, k), lambda i, j: (i, 0)),
                  pl.BlockSpec((k, tn), lambda i, j: (0, j))],
        out_specs=pl.BlockSpec((tm, tn), lambda i, j: (i, j)),
        out_shape=jax.ShapeDtypeStruct((m, n), out_dtype),
        compiler_params=_cparams(("parallel", "parallel")),
        name="proj_matmul",
    )(x, w)


def _compress_kernel(ch_ref, pos_ref, w1_ref, w2_ref, o_ref):
    rows = lax.broadcasted_iota(jnp.int32, (N_CHUNK, 1), 0)
    for pair in range(2):
        pieces = []
        for kv in range(2):
            for hf in range(2):
                g = 2 * pair + hf
                c = ch_ref[kv * N_KV + g].astype(F32)
                a_in = (c + pos_ref[kv, 0]).astype(BF16)
                b_in = (c + pos_ref[kv, 1]).astype(BF16)
                a = jnp.dot(a_in, w1_ref[kv, 0], preferred_element_type=F32)
                b = jnp.dot(b_in, w1_ref[kv, 1], preferred_element_type=F32)
                pre = a + pltpu.roll(b, N_CHUNK - 1, 0)
                hid = _gelu_tanh(pre).astype(BF16)
                out = jnp.dot(hid, w2_ref[kv], preferred_element_type=F32)
                out = jnp.where(rows < N_CHUNK - 1, out, 0.0)
                pieces.append(out)
        o_ref[pair] = jnp.concatenate(pieces, axis=1).astype(o_ref.dtype)


def _compress(chunks, pos, w1, w2):
    bsz = chunks.shape[0]
    return pl.pallas_call(
        _compress_kernel,
        grid=(bsz,),
        in_specs=[pl.BlockSpec((None, 8, N_CHUNK, 1024), lambda b: (b, 0, 0, 0)),
                  pl.BlockSpec((2, 2, 1, 1024), lambda b: (0, 0, 0, 0)),
                  pl.BlockSpec((2, 2, 1024, CMP_HIDDEN), lambda b: (0, 0, 0, 0)),
                  pl.BlockSpec((2, CMP_HIDDEN, HEAD_DIM), lambda b: (0, 0, 0))],
        out_specs=pl.BlockSpec((None, 2, N_CHUNK, 256), lambda b: (b, 0, 0, 0)),
        out_shape=jax.ShapeDtypeStruct((bsz, 2, N_CHUNK, 256), BF16),
        compiler_params=_cparams(("parallel",)),
        name="nsa_compress",
    )(chunks, pos, w1, w2)


def _softmax_rows(s, valid):
    m = jnp.max(s, axis=1, keepdims=True)
    p = jnp.where(valid, jnp.exp(s - m), 0.0)
    l = jnp.sum(p, axis=1, keepdims=True)
    return p * jnp.where(l > 0.0, 1.0 / l, 0.0)


def _attn_kernel(slopes_ref, q_ref, kcv_ref, ks_ref, vs_ref, kw_ref, vw_ref, g_ref, ov_ref,
                 o_ref, m_ref, l_ref, acc_ref):
    gp = pl.program_id(1)
    qb = pl.program_id(2)
    t0 = qb * Q_BLOCK
    n_q4 = HPG * Q_BLOCK
    tq = t0 + lax.broadcasted_iota(jnp.int32, (Q_BLOCK, 1), 0)
    tq4 = jnp.concatenate([tq] * HPG, axis=0)
    qblk = q_ref[...]
    gates = g_ref[...]
    zeros64 = jnp.zeros((Q_BLOCK, HEAD_DIM), BF16)
    outs = []
    for hf in range(2):
        g = 2 * gp + hf
        parts = []
        slope_parts = []
        for hg in range(HPG):
            h = hf * HPG + hg
            qh = (qblk[:, h * HEAD_DIM:(h + 1) * HEAD_DIM].astype(F32) * (HEAD_DIM ** -0.5)).astype(BF16)
            parts.append(jnp.concatenate([qh, zeros64] if hf == 0 else [zeros64, qh], axis=1))
            slope_parts.append(jnp.full((Q_BLOCK, 1), slopes_ref[g * HPG + hg], F32))
        q4 = jnp.concatenate(parts, axis=0)
        slope = jnp.concatenate(slope_parts, axis=0)

        kcp = kcv_ref[:, 0:LANES]
        vcp = kcv_ref[:, LANES:2 * LANES]
        cend = lax.broadcasted_iota(jnp.int32, (1, N_CHUNK), 1) * CMP_STRIDE + (CMP_LEN - 1)
        valid_c = cend <= tq4
        s = _dot_nt(q4, kcp) - slope * (tq4 - cend).astype(F32)
        p_c = _softmax_rows(jnp.where(valid_c, s, NEG), valid_c)
        o_cmp = jnp.dot(p_c.astype(BF16), vcp, preferred_element_type=F32)

        psum = p_c[0:Q_BLOCK]
        for hg in range(1, HPG):
            psum = psum + p_c[hg * Q_BLOCK:(hg + 1) * Q_BLOCK]
        imp = jnp.dot(psum, ov_ref[...], preferred_element_type=F32,
                      precision=lax.Precision.HIGHEST)
        n_slc = imp.shape[1]
        jb = lax.broadcasted_iota(jnp.int32, (Q_BLOCK, n_slc), 1)
        cur = tq >> 6
        forced = (jb == 0) | (jb == cur) | (jb == cur - 1)
        imp = jnp.where(forced, imp + FORCE_BONUS, imp)
        imp = jnp.where(jb <= cur, imp, -1.0)
        rank = jnp.zeros((Q_BLOCK, n_slc), F32)
        for i in range(n_slc):
            ci = imp[:, i:i + 1]
            ge = jnp.where(ci >= imp, 1.0, 0.0)
            gt = jnp.where(ci > imp, 1.0, 0.0)
            rank = rank + jnp.where(jb > i, ge, gt)
        sel = jnp.where(rank < float(N_SLC_SEL), 1.0, 0.0).astype(BF16)

        m_ref[...] = jnp.full((n_q4, LANES), NEG, F32)
        l_ref[...] = jnp.zeros((n_q4, LANES), F32)
        acc_ref[...] = jnp.zeros((n_q4, LANES), F32)
        blocks_per_tile = SLC_KEY_TILE // SLC_LEN

        def slc_body(kt, carry):
            k0 = pl.multiple_of(kt * SLC_KEY_TILE, SLC_KEY_TILE)
            k_tile = ks_ref[pl.ds(k0, SLC_KEY_TILE), :]
            v_tile = vs_ref[pl.ds(k0, SLC_KEY_TILE), :]
            kpos = k0 + lax.broadcasted_iota(jnp.int32, (1, SLC_KEY_TILE), 1)
            jj = lax.broadcasted_iota(jnp.int32, (n_slc, SLC_KEY_TILE), 0)
            kk = lax.broadcasted_iota(jnp.int32, (n_slc, SLC_KEY_TILE), 1)
            expand = jnp.where((kk >> 6) + kt * blocks_per_tile == jj, 1.0, 0.0).astype(BF16)
            mexp = jnp.dot(sel, expand, preferred_element_type=F32)
            vf = jnp.where(kpos <= tq, mexp, 0.0)
            valid = jnp.concatenate([vf] * HPG, axis=0) > 0.5
            s_t = _dot_nt(q4, k_tile) + slope * (kpos - tq4).astype(F32)
            s_t = jnp.where(valid, s_t, NEG)
            m_prev = m_ref[...]
            m_cur = jnp.max(s_t, axis=1, keepdims=True)
            m_next = jnp.maximum(m_prev, m_cur)
            p = jnp.where(valid, jnp.exp(s_t - m_next[:, 0:1]), 0.0)
            alpha = jnp.exp(m_prev - m_next)
            l_ref[...] = alpha * l_ref[...] + jnp.sum(p, axis=1, keepdims=True)
            acc_ref[...] = alpha * acc_ref[...] + jnp.dot(p.astype(BF16), v_tile,
                                                          preferred_element_type=F32)
            m_ref[...] = m_next
            return carry

        lax.fori_loop(0, qb // (SLC_KEY_TILE // Q_BLOCK) + 1, slc_body, 0)
        l_s = l_ref[...]
        o_slc = acc_ref[...] * jnp.where(l_s > 0.0, 1.0 / l_s, 0.0)

        n_win = WINDOW + Q_BLOCK
        w0 = pl.multiple_of(jnp.maximum(t0 - WINDOW, 0), Q_BLOCK)
        kw_tile = kw_ref[pl.ds(w0, n_win), :]
        vw_tile = vw_ref[pl.ds(w0, n_win), :]
        dw = tq4 - (w0 + lax.broadcasted_iota(jnp.int32, (1, n_win), 1))
        valid_w = (dw >= 0) & (dw < WINDOW)
        s_w = _dot_nt(q4, kw_tile) - slope * dw.astype(F32)
        p_w = _softmax_rows(jnp.where(valid_w, s_w, NEG), valid_w)
        o_win = jnp.dot(p_w.astype(BF16), vw_tile, preferred_element_type=F32)

        for hg in range(HPG):
            h = hf * HPG + hg
            rs = slice(hg * Q_BLOCK, (hg + 1) * Q_BLOCK)
            cs = slice(hf * HEAD_DIM, (hf + 1) * HEAD_DIM)
            outs.append(gates[:, 3 * h:3 * h + 1] * o_cmp[rs, cs]
                        + gates[:, 3 * h + 1:3 * h + 2] * o_slc[rs, cs]
                        + gates[:, 3 * h + 2:3 * h + 3] * o_win[rs, cs])
    o_ref[...] = jnp.concatenate(outs, axis=1).astype(o_ref.dtype)


def _nsa_attention(slopes, qkv, kcv, gates, overlap):
    bsz, seq, _ = qkv.shape
    pair_w = HEADS_PER_PAIR * HEAD_DIM
    kv0 = Q_COLS // LANES
    blk = N_KV * HEAD_DIM // LANES

    def kv_spec(idx):
        return pl.BlockSpec((None, seq, LANES), lambda b, gp, qb: (b, 0, kv0 + idx * blk + gp))

    n_q4 = HPG * Q_BLOCK
    return pl.pallas_call(
        _attn_kernel,
        grid=(bsz, 2, seq // Q_BLOCK),
        in_specs=[pl.BlockSpec(memory_space=pltpu.SMEM),
                  pl.BlockSpec((None, Q_BLOCK, pair_w), lambda b, gp, qb: (b, qb, gp)),
                  pl.BlockSpec((None, None, N_CHUNK, 256), lambda b, gp, qb: (b, gp, 0, 0)),
                  kv_spec(2), kv_spec(3), kv_spec(4), kv_spec(5),
                  pl.BlockSpec((None, Q_BLOCK, LANES), lambda b, gp, qb: (b, qb, gp)),
                  pl.BlockSpec((N_CHUNK, seq // SLC_LEN), lambda b, gp, qb: (0, 0))],
        out_specs=pl.BlockSpec((None, Q_BLOCK, pair_w), lambda b, gp, qb: (b, qb, gp)),
        out_shape=jax.ShapeDtypeStruct((bsz, seq, Q_COLS), BF16),
        scratch_shapes=[pltpu.VMEM((n_q4, LANES), F32)] * 3,
        compiler_params=_cparams(("parallel", "parallel", "arbitrary")),
        name="nsa_attention",
    )(slopes, qkv, kcv, qkv, qkv, qkv, qkv, gates, overlap)


LOG2E = float(np.log2(np.e))
POS_LANE0 = SLC_LEN


def _attn2_kernel(q_ref, kcv_ref, ks_ref, vs_ref, kw_ref, vw_ref, g_ref, ovt_ref, kaug_ref, kaugc_ref,
                  stab_ref, o_ref, m_ref, l_ref, acc_ref, qs_ref, qp_ref, ocmp_ref):
    gp = pl.program_id(1)
    qb = pl.program_id(2)
    t0 = qb * Q_BLOCK
    n_q4 = HPG * Q_BLOCK
    n_win = WINDOW + Q_BLOCK
    n_slc = ovt_ref.shape[0]
    tq = t0 + lax.broadcasted_iota(jnp.int32, (Q_BLOCK, 1), 0)
    lane = lax.broadcasted_iota(jnp.int32, (Q_BLOCK, LANES), 1)
    lane_row = lax.broadcasted_iota(jnp.int32, (1, LANES), 1)
    qblk = q_ref[...]
    gates = g_ref[...]
    zeros64 = jnp.zeros((Q_BLOCK, HEAD_DIM), BF16)

    cend = lax.broadcasted_iota(jnp.int32, (1, N_CHUNK), 1) * CMP_STRIDE + (CMP_LEN - 1)
    mask_c = jnp.where(cend <= tq, 0.0, NEG)[None]
    w0 = pl.multiple_of(jnp.maximum(t0 - WINDOW, 0), Q_BLOCK)
    dw = tq - (w0 + lax.broadcasted_iota(jnp.int32, (1, n_win), 1))
    mask_w = jnp.where((dw >= 0) & (dw < WINDOW), 0.0, NEG)[None]
    kt_last = qb // (SLC_KEY_TILE // Q_BLOCK)
    k0_last = pl.multiple_of(kt_last * SLC_KEY_TILE, SLC_KEY_TILE)
    kpos_l = k0_last + lax.broadcasted_iota(jnp.int32, (1, SLC_KEY_TILE), 1)
    mask_d = jnp.where(kpos_l <= tq, 0.0, NEG)[None]
    row_ok = jnp.where(tq >= CMP_LEN - 1, 1.0, 0.0)[None]

    jb_t = lax.broadcasted_iota(jnp.int32, (n_slc, Q_BLOCK), 0)
    cur_t = (t0 + lax.broadcasted_iota(jnp.int32, (1, Q_BLOCK), 1)) >> 6
    forced_t = (jb_t == 0) | (jb_t == cur_t) | (jb_t == cur_t - 1)
    sub8 = lax.broadcasted_iota(jnp.int32, (8, Q_BLOCK), 0)

    kaug_c = kaugc_ref[...]
    for hf in range(2):
        g = 2 * gp + hf
        q_parts, pos_parts = [], []
        for hg in range(HPG):
            h = hf * HPG + hg
            qh = (qblk[:, h * HEAD_DIM:(h + 1) * HEAD_DIM].astype(F32) * (HEAD_DIM ** -0.5 * LOG2E)).astype(BF16)
            q_parts.append(jnp.concatenate([qh, zeros64] if hf == 0 else [zeros64, qh], axis=1))
            srow = stab_ref[pl.ds(g * HPG + hg, 1), :]
            hi = srow.astype(BF16).astype(F32)
            mid = (srow - hi).astype(BF16).astype(F32)
            lo = (srow - hi - mid).astype(BF16).astype(F32)
            pos_parts.append(jnp.where(lane_row < POS_LANE0 + 2, hi,
                                       jnp.where(lane_row < POS_LANE0 + 4, mid, lo)))
        q4 = jnp.concatenate(q_parts, axis=0)
        pos_plain = jnp.concatenate([jnp.broadcast_to(p, (Q_BLOCK, LANES)) for p in pos_parts], axis=0)
        q_plain = jnp.concatenate([q4, pos_plain.astype(BF16)], axis=1)

        k_c = jnp.concatenate([kcv_ref[:, 0:LANES], kaug_c], axis=1)
        s3 = _dot_nt(q_plain, k_c).reshape(HPG, Q_BLOCK, N_CHUNK) + mask_c
        p3 = jnp.exp2(s3 - jnp.max(s3, axis=2, keepdims=True))
        l3 = jnp.sum(p3, axis=2, keepdims=True)
        p3 = p3 * (row_ok / l3)
        o_cmp = jnp.dot(p3.reshape(n_q4, N_CHUNK).astype(BF16), kcv_ref[:, LANES:2 * LANES],
                        preferred_element_type=F32)

        psum = p3[0]
        for hg in range(1, HPG):
            psum = psum + p3[hg]
        imp = lax.dot_general(ovt_ref[...], psum, (((1,), (1,)), ((), ())), preferred_element_type=F32,
                              precision=lax.Precision.HIGHEST)
        imp = jnp.where(forced_t, imp + FORCE_BONUS, imp)
        imp = jnp.where(jb_t <= cur_t, imp, -1.0)
        xs = [imp[8 * r:8 * r + 8, :] for r in range(n_slc // 8)]
        ranks = [jnp.zeros((8, Q_BLOCK), F32) for _ in xs]
        for i in range(n_slc):
            ri, si = divmod(i, 8)
            row = xs[ri][si:si + 1, :]
            for r in range(len(xs)):
                if r > ri:
                    hit = jnp.where(row >= xs[r], 1.0, 0.0)
                elif r < ri:
                    hit = jnp.where(row > xs[r], 1.0, 0.0)
                else:
                    hit = jnp.where(sub8 > si, jnp.where(row >= xs[r], 1.0, 0.0),
                                    jnp.where(row > xs[r], 1.0, 0.0))
                ranks[r] = ranks[r] + hit
        sel_t = jnp.concatenate([jnp.where(rk < float(N_SLC_SEL), 1.0, 0.0) for rk in ranks]
                                + [jnp.zeros((LANES - n_slc, Q_BLOCK), F32)], axis=0)
        sel = sel_t.T
        mask_lanes = (sel - 1.0) * (-NEG)
        pos_sel = jnp.concatenate([jnp.where(lane < POS_LANE0, mask_lanes, jnp.broadcast_to(p, (Q_BLOCK, LANES)))
                                   for p in pos_parts], axis=0)
        qs_ref[hf] = jnp.concatenate([q4, pos_sel.astype(BF16)], axis=1)
        qp_ref[hf] = q_plain
        ocmp_ref[hf] = o_cmp

    m_ref[...] = jnp.full(m_ref.shape, NEG, F32)
    l_ref[...] = jnp.zeros(l_ref.shape, F32)
    acc_ref[...] = jnp.zeros(acc_ref.shape, F32)

    def slc_tile(k0, mask3):
        kaug_t = kaug_ref[pl.ds(k0, SLC_KEY_TILE), :]
        k2 = jnp.concatenate([ks_ref[pl.ds(k0, SLC_KEY_TILE), :], kaug_t], axis=1)
        v_t = vs_ref[pl.ds(k0, SLC_KEY_TILE), :]
        for hf in range(2):
            s_t = _dot_nt(qs_ref[hf], k2)
            if mask3 is not None:
                s_t = (s_t.reshape(HPG, Q_BLOCK, SLC_KEY_TILE) + mask3).reshape(n_q4, SLC_KEY_TILE)
            m_prev = m_ref[hf]
            m_next = jnp.maximum(m_prev, jnp.max(s_t, axis=1, keepdims=True))
            p = jnp.exp2(s_t - m_next[:, 0:1])
            alpha = jnp.exp2(m_prev - m_next)
            l_ref[hf] = alpha * l_ref[hf] + jnp.sum(p, axis=1, keepdims=True)
            acc_ref[hf] = alpha * acc_ref[hf] + jnp.dot(p.astype(BF16), v_t, preferred_element_type=F32)
            m_ref[hf] = m_next

    def slc_body(kt, carry):
        slc_tile(pl.multiple_of(kt * SLC_KEY_TILE, SLC_KEY_TILE), None)
        return carry

    lax.fori_loop(0, kt_last, slc_body, 0)
    slc_tile(k0_last, mask_d)

    k_w = jnp.concatenate([kw_ref[pl.ds(w0, n_win), :], kaug_ref[pl.ds(w0, n_win), :]], axis=1)
    v_w = vw_ref[pl.ds(w0, n_win), :]
    outs = []
    for hf in range(2):
        o_cmp = ocmp_ref[hf]
        o_slc = acc_ref[hf] * (1.0 / l_ref[hf])
        s3 = _dot_nt(qp_ref[hf], k_w).reshape(HPG, Q_BLOCK, n_win) + mask_w
        p3 = jnp.exp2(s3 - jnp.max(s3, axis=2, keepdims=True))
        l_w = jnp.sum(p3, axis=2, keepdims=True).reshape(n_q4, 1)
        o_win = jnp.dot(p3.reshape(n_q4, n_win).astype(BF16), v_w, preferred_element_type=F32) * (1.0 / l_w)
        for hg in range(HPG):
            h = hf * HPG + hg
            rs = slice(hg * Q_BLOCK, (hg + 1) * Q_BLOCK)
            cs = slice(hf * HEAD_DIM, (hf + 1) * HEAD_DIM)
            outs.append(gates[:, 3 * h:3 * h + 1] * o_cmp[rs, cs]
                        + gates[:, 3 * h + 1:3 * h + 2] * o_slc[rs, cs]
                        + gates[:, 3 * h + 2:3 * h + 3] * o_win[rs, cs])
    o_ref[...] = jnp.concatenate(outs, axis=1).astype(o_ref.dtype)


def _position_lanes(pos, with_block):
    ext = np.zeros((pos.shape[0], LANES), np.float32)
    blk = pos // SLC_LEN
    if with_block:
        ext[np.arange(pos.shape[0]), blk] = 1.0
    for k in range(3):
        ext[:, POS_LANE0 + 2 * k] = blk * SLC_LEN
        ext[:, POS_LANE0 + 2 * k + 1] = pos % SLC_LEN
    return jnp.asarray(ext, BF16)


def _nsa_attention2(qkv, kcv, gates):
    bsz, seq, _ = qkv.shape
    pair_w = HEADS_PER_PAIR * HEAD_DIM
    kv0 = Q_COLS // LANES
    blk = N_KV * HEAD_DIM // LANES
    n_slc = seq // SLC_LEN
    n_cmp = (seq - CMP_LEN) // CMP_STRIDE + 1

    cmp_start = np.arange(N_CHUNK) * CMP_STRIDE
    slc_start = np.arange(n_slc) * SLC_LEN
    ov_t = ((cmp_start[None, :] < slc_start[:, None] + SLC_LEN)
            & (cmp_start[None, :] + CMP_LEN > slc_start[:, None])
            & (np.arange(N_CHUNK)[None, :] < n_cmp)).astype(np.float32)
    kaug = _position_lanes(np.arange(seq), True)
    kaug_c = _position_lanes(cmp_start + CMP_LEN - 1, False)
    slopes = (2.0 ** (-8.0 * np.arange(1, N_HEADS + 1) / N_HEADS)).astype(np.float32)
    stab = np.zeros((N_HEADS, LANES), np.float32)
    stab[:, POS_LANE0:POS_LANE0 + 6] = (slopes * np.float32(LOG2E))[:, None]

    def kv_spec(idx):
        return pl.BlockSpec((None, seq, LANES), lambda b, gp, qb: (b, 0, kv0 + idx * blk + gp))

    def const_spec(shape):
        return pl.BlockSpec(shape, lambda b, gp, qb: (0,) * len(shape))

    n_q4 = HPG * Q_BLOCK
    return pl.pallas_call(
        _attn2_kernel,
        grid=(bsz, 2, seq // Q_BLOCK),
        in_specs=[pl.BlockSpec((None, Q_BLOCK, pair_w), lambda b, gp, qb: (b, qb, gp)),
                  pl.BlockSpec((None, None, N_CHUNK, 256), lambda b, gp, qb: (b, gp, 0, 0)),
                  kv_spec(2), kv_spec(3), kv_spec(4), kv_spec(5),
                  pl.BlockSpec((None, Q_BLOCK, LANES), lambda b, gp, qb: (b, qb, gp)),
                  const_spec((n_slc, N_CHUNK)), const_spec((seq, LANES)), const_spec((N_CHUNK, LANES)),
                  const_spec((N_HEADS, LANES))],
        out_specs=pl.BlockSpec((None, Q_BLOCK, pair_w), lambda b, gp, qb: (b, qb, gp)),
        out_shape=jax.ShapeDtypeStruct((bsz, seq, Q_COLS), BF16),
        scratch_shapes=[pltpu.VMEM((2, n_q4, LANES), F32)] * 3
        + [pltpu.VMEM((2, n_q4, 2 * LANES), BF16)] * 2 + [pltpu.VMEM((2, n_q4, LANES), F32)],
        compiler_params=_cparams(("parallel", "parallel", "arbitrary")),
        name="nsa_attention",
    )(qkv, kcv, qkv, qkv, qkv, qkv, gates, jnp.asarray(ov_t), kaug, kaug_c, jnp.asarray(stab))


CONV_TILE = 256
CONV_HALO = 32
CONV_CHUNK = 32


def _conv_kernel(ab_ref, halo_ref, w_ref, b_ref, g_ref, beta_ref, o_ref, u_ref):
    i = pl.program_id(1)
    ab = ab_ref[...]
    u_ref[CONV_HALO:, :] = ab[:, :D_CONV] * jax.nn.sigmoid(ab[:, D_CONV:])
    hb = halo_ref[...]
    uh = hb[:, :D_CONV] * jax.nn.sigmoid(hb[:, D_CONV:])
    u_ref[0:CONV_HALO, :] = jnp.where(i > 0, uh, 0.0)
    w = w_ref[...]
    off = CONV_HALO - (CONV_WIDTH - 1)

    for c in range(CONV_TILE // CONV_CHUNK):
        r0 = c * CONV_CHUNK
        acc = jnp.zeros((CONV_CHUNK, D_CONV), F32) + b_ref[...]
        for k in range(CONV_WIDTH):
            acc = acc + u_ref[r0 + off + k:r0 + off + k + CONV_CHUNK, :] * w[k:k + 1, :]
        y = _layer_norm_rows(acc, g_ref[...], beta_ref[...])
        o_ref[r0:r0 + CONV_CHUNK, :] = (y * jax.nn.sigmoid(y)).astype(o_ref.dtype)


def _conv_group(conv_ab, dw_w, dw_b, ln_g, ln_b):
    bsz, seq, width = conv_ab.shape
    ratio = CONV_TILE // CONV_HALO
    return pl.pallas_call(
        _conv_kernel,
        grid=(bsz, seq // CONV_TILE),
        in_specs=[pl.BlockSpec((None, CONV_TILE, width), lambda b, i: (b, i, 0)),
                  pl.BlockSpec((None, CONV_HALO, width), lambda b, i: (b, jnp.maximum(i * ratio - 1, 0), 0)),
                  pl.BlockSpec((CONV_WIDTH, D_CONV), lambda b, i: (0, 0)),
                  pl.BlockSpec((1, D_CONV), lambda b, i: (0, 0)),
                  pl.BlockSpec((1, D_CONV), lambda b, i: (0, 0)),
                  pl.BlockSpec((1, D_CONV), lambda b, i: (0, 0))],
        out_specs=pl.BlockSpec((None, CONV_TILE, D_CONV), lambda b, i: (b, i, 0)),
        out_shape=jax.ShapeDtypeStruct((bsz, seq, D_CONV), BF16),
        scratch_shapes=[pltpu.VMEM((CONV_HALO + CONV_TILE, D_CONV), F32)],
        compiler_params=_cparams(("parallel", "parallel")),
        name="conformer_conv",
    )(conv_ab, conv_ab, dw_w, dw_b, ln_g, ln_b)


def _outproj_kernel(on_ref, oc_ref, x_ref, w_ref, g_ref, b_ref, y_ref, yt_ref):
    k_nsa = on_ref.shape[1]
    mix = jnp.dot(on_ref[...], w_ref[0:k_nsa, :], preferred_element_type=F32)
    mix = mix + jnp.dot(oc_ref[...], w_ref[k_nsa:, :], preferred_element_type=F32)
    y = _layer_norm_rows(DN_ALPHA * x_ref[...] + mix, g_ref[...], b_ref[...])
    y_ref[...] = y
    yt_ref[...] = y.T.astype(yt_ref.dtype)


def _outproj_ln(o_nsa, o_conv, x, w_out, g, b, tm=256):
    m, d = x.shape
    return pl.pallas_call(
        _outproj_kernel,
        grid=(m // tm,),
        in_specs=[pl.BlockSpec((tm, o_nsa.shape[1]), lambda i: (i, 0)),
                  pl.BlockSpec((tm, o_conv.shape[1]), lambda i: (i, 0)),
                  pl.BlockSpec((tm, d), lambda i: (i, 0)),
                  pl.BlockSpec(w_out.shape, lambda i: (0, 0)),
                  pl.BlockSpec((1, d), lambda i: (0, 0)),
                  pl.BlockSpec((1, d), lambda i: (0, 0))],
        out_specs=[pl.BlockSpec((tm, d), lambda i: (i, 0)),
                   pl.BlockSpec((d, tm), lambda i: (0, i))],
        out_shape=[jax.ShapeDtypeStruct((m, d), F32),
                   jax.ShapeDtypeStruct((d, m), BF16)],
        compiler_params=_cparams(("parallel",)),
        name="outproj_ln1",
    )(o_nsa, o_conv, x, w_out, g, b)


def _sort_pairs(n):
    pairs = []
    p = 1
    while p < n:
        k = p
        while k >= 1:
            for j in range(k % p, n - k, 2 * k):
                for i in range(min(k, n - j - k)):
                    if (i + j) // (2 * p) == (i + j + k) // (2 * p):
                        pairs.append((i + j, i + j + k))
            k //= 2
        p *= 2
    return pairs


_SORT16 = _sort_pairs(PK_TOPK)
_CAND = [(i, j) for i in range(PK_TOPK) for j in range(PK_TOPK) if (i + 1) * (j + 1) <= PK_TOPK]


def _sort_desc(xs):
    xs = list(xs)
    for (i, j) in _SORT16:
        hi = jnp.maximum(xs[i], xs[j])
        lo = jnp.minimum(xs[i], xs[j])
        xs[i], xs[j] = hi, lo
    return xs


def _merge_top(xs, ys):
    n = len(xs)
    zs = [jnp.maximum(xs[i], ys[n - 1 - i]) for i in range(n)]
    k = n // 2
    while k >= 1:
        for i in range(n):
            if (i & k) == 0:
                hi = jnp.maximum(zs[i], zs[i + k])
                lo = jnp.minimum(zs[i], zs[i + k])
                zs[i], zs[i + k] = hi, lo
        k //= 2
    return zs


def _peer_score_kernel(yt_ref, wqt_ref, keys_ref, s1_ref, e1_ref, s2_ref, e2_ref, tau_ref):
    tm = yt_ref.shape[1]
    qt = jnp.dot(wqt_ref[...], yt_ref[...], preferred_element_type=F32)
    sub = lax.broadcasted_iota(jnp.int32, (8, tm), 0)
    tops = [[None] * PK_TOPK, [None] * PK_TOPK]
    for hh in range(2 * PEER_HEADS):
        h, half = hh // 2, hh % 2
        qh = qt[hh * N_KEYS:(hh + 1) * N_KEYS, :].astype(BF16)
        st = jnp.dot(keys_ref[hh], qh, preferred_element_type=F32)
        (s1_ref if half == 0 else s2_ref)[h] = st
        xs = _sort_desc([st[8 * r:8 * r + 8, :] for r in range(N_KEYS // 8)])
        for sh in (4, 2, 1):
            xs = _merge_top(xs, [pltpu.roll(x, sh, 0) for x in xs])
        for i in range(PK_TOPK):
            tops[half][i] = xs[i] if h == 0 else jnp.where(sub == h, xs[i], tops[half][i])
    v1, v2 = tops
    cands = [v1[i] + v2[j] for (i, j) in _CAND]
    c16 = None
    for ca in cands:
        cnt = jnp.zeros_like(ca)
        for cb in cands:
            cnt = cnt + jnp.where(cb >= ca, 1.0, 0.0)
        cand_ok = jnp.where(cnt >= float(PK_TOPK), ca, -jnp.inf)
        c16 = cand_ok if c16 is None else jnp.maximum(c16, cand_ok)
    mx = cands[0]
    z = jnp.zeros_like(mx)
    for ca in cands:
        z = z + jnp.where(ca >= c16, jnp.exp(ca - mx), 0.0)
    zinv = 1.0 / z
    tau_ref[...] = c16
    for h in range(PEER_HEADS):
        e1_ref[h] = jnp.exp(s1_ref[h] - v1[0][h:h + 1, :])
        e2_ref[h] = jnp.exp(s2_ref[h] - v2[0][h:h + 1, :]) * zinv[h:h + 1, :]


def _peer_scores(yt, wqt, keys, tm=256):
    d, m = yt.shape
    big = jax.ShapeDtypeStruct((PEER_HEADS, N_KEYS, m), F32)
    big_spec = pl.BlockSpec((PEER_HEADS, N_KEYS, tm), lambda i: (0, 0, i))
    return pl.pallas_call(
        _peer_score_kernel,
        grid=(m // tm,),
        in_specs=[pl.BlockSpec((d, tm), lambda i: (0, i)),
                  pl.BlockSpec(wqt.shape, lambda i: (0, 0)),
                  pl.BlockSpec(keys.shape, lambda i: (0, 0, 0))],
        out_specs=[big_spec, big_spec, big_spec, big_spec,
                   pl.BlockSpec((PEER_HEADS, tm), lambda i: (0, i))],
        out_shape=[big, big, big, big, jax.ShapeDtypeStruct((PEER_HEADS, m), F32)],
        compiler_params=_cparams(("parallel",)),
        name="peer_scores",
    )(yt, wqt, keys)


PEER_TM = 512
PEER_TE = 512
PEER_MXU_N = 256
PEER_VPU_ROWS = 32

def _peer_mix_kernel(yt_ref, u_ref, vt_ref, s1_ref, e1_ref, s2_ref, e2_ref, tau_ref, o_ref,
                     hid_a, hid_b, g_a, g_b, *, n_tiles):
    j = pl.program_id(1)
    n_a = PEER_TE // N_KEYS

    @pl.when(j == 0)
    def _():
        o_ref[...] = jnp.zeros_like(o_ref)
        for r in (hid_a, hid_b, g_a, g_b):
            r[...] = jnp.zeros_like(r)

    def stage(hid_w, hid_r, g_w, g_r):
        a0 = jnp.clip(j - 1, 0, n_tiles - 1) * n_a
        n_tk = PEER_TM // PEER_MXU_N
        n_mm = n_tk * n_a
        d_rows = D_MODEL // n_mm
        n_lc = PEER_TM // LANES
        n_rq = N_KEYS // PEER_VPU_ROWS
        s1_rows = [[s1_ref[h, pl.ds(a0 + ai, 1), :] for h in range(PEER_HEADS)] for ai in range(n_a)]
        e1_rows = [[e1_ref[h, pl.ds(a0 + ai, 1), :] for h in range(PEER_HEADS)] for ai in range(n_a)]
        for item in range(n_lc * n_rq):
            lc, rq = divmod(item, n_rq)
            ls = slice(lc * LANES, (lc + 1) * LANES)
            rs = slice(rq * PEER_VPU_ROWS, (rq + 1) * PEER_VPU_ROWS)
            acc = [None] * n_a
            for h in range(PEER_HEADS):
                s2v = s2_ref[h, rs, ls]
                e2v = e2_ref[h, rs, ls]
                tau_h = tau_ref[h:h + 1, ls]
                for ai in range(n_a):
                    contrib = jnp.where(s1_rows[ai][h][:, ls] + s2v >= tau_h, e1_rows[ai][h][:, ls] * e2v, 0.0)
                    acc[ai] = contrib if acc[ai] is None else acc[ai] + contrib
            for ai in range(n_a):
                er = slice(ai * N_KEYS + rq * PEER_VPU_ROWS, ai * N_KEYS + (rq + 1) * PEER_VPU_ROWS)
                g_w[er, ls] = (acc[ai] * _gelu_tanh(hid_r[er, ls])).astype(BF16)
            piece = item // 2
            if item % 2 == 0:
                tk, ai = divmod(piece, n_a)
                rows = slice(ai * N_KEYS, (ai + 1) * N_KEYS)
                lanes = slice(tk * PEER_MXU_N, (tk + 1) * PEER_MXU_N)
                hid_w[rows, lanes] = jnp.dot(u_ref[rows, :], yt_ref[:, lanes], preferred_element_type=F32)
            else:
                dr = slice(piece * d_rows, (piece + 1) * d_rows)
                o_ref[dr, :] += jnp.dot(vt_ref[dr, :], g_r[...], preferred_element_type=F32)

    @pl.when(j % 2 == 0)
    def _():
        stage(hid_a, hid_b, g_b, g_a)

    @pl.when(j % 2 == 1)
    def _():
        stage(hid_b, hid_a, g_a, g_b)


def _peer_mix(yt, u, vt, s1, e1, s2, e2, tau):
    d, m = yt.shape
    n_tiles = u.shape[0] // PEER_TE
    big_spec = pl.BlockSpec((PEER_HEADS, N_KEYS, PEER_TM), lambda i, j: (0, 0, i))
    return pl.pallas_call(
        functools.partial(_peer_mix_kernel, n_tiles=n_tiles),
        grid=(m // PEER_TM, n_tiles + 2),
        in_specs=[pl.BlockSpec((d, PEER_TM), lambda i, j: (0, i)),
                  pl.BlockSpec((PEER_TE, d), lambda i, j: (jnp.minimum(j, n_tiles - 1), 0)),
                  pl.BlockSpec((d, PEER_TE), lambda i, j: (0, jnp.clip(j - 2, 0, n_tiles - 1))),
                  big_spec, big_spec, big_spec, big_spec,
                  pl.BlockSpec((PEER_HEADS, PEER_TM), lambda i, j: (0, i))],
        out_specs=pl.BlockSpec((d, PEER_TM), lambda i, j: (0, i)),
        out_shape=jax.ShapeDtypeStruct((d, m), F32),
        scratch_shapes=[pltpu.VMEM((PEER_TE, PEER_TM), F32), pltpu.VMEM((PEER_TE, PEER_TM), F32),
                        pltpu.VMEM((PEER_TE, PEER_TM), BF16), pltpu.VMEM((PEER_TE, PEER_TM), BF16)],
        compiler_params=_cparams(("parallel", "arbitrary")),
        name="peer_mix",
    )(yt, u, vt, s1, e1, s2, e2, tau)


def _final_ln_kernel(ft_ref, y_ref, g_ref, b_ref, o_ref):
    z = DN_ALPHA * y_ref[...] + ft_ref[...].T
    o_ref[...] = _layer_norm_rows(z, g_ref[...], b_ref[...])


def _final_ln(ffn_t, y, g, b, tm=256):
    m, d = y.shape
    return pl.pallas_call(
        _final_ln_kernel,
        grid=(m // tm,),
        in_specs=[pl.BlockSpec((d, tm), lambda i: (0, i)),
                  pl.BlockSpec((tm, d), lambda i: (i, 0)),
                  pl.BlockSpec((1, d), lambda i: (0, 0)),
                  pl.BlockSpec((1, d), lambda i: (0, 0))],
        out_specs=pl.BlockSpec((tm, d), lambda i: (i, 0)),
        out_shape=jax.ShapeDtypeStruct((m, d), F32),
        compiler_params=_cparams(("parallel",)),
        name="final_ln2",
    )(ffn_t, y, g, b)


def _alibi_slopes():
    return jnp.asarray(2.0 ** (-8.0 * np.arange(1, N_HEADS + 1) / N_HEADS), F32)


def _overlap_matrix(seq):
    n_cmp = (seq - CMP_LEN) // CMP_STRIDE + 1
    n_slc = seq // SLC_LEN
    cmp_start = np.arange(N_CHUNK) * CMP_STRIDE
    slc_start = np.arange(n_slc) * SLC_LEN
    ov = ((cmp_start[:, None] < slc_start[None, :] + SLC_LEN)
          & (cmp_start[:, None] + CMP_LEN > slc_start[None, :])
          & (np.arange(N_CHUNK)[:, None] < n_cmp)).astype(np.float32)
    return jnp.asarray(ov)


def _layer(x, w_in, cmp_pos_k, cmp_w1_k, cmp_w2_k, cmp_pos_v, cmp_w1_v, cmp_w2_v,
           dw_w, dw_b, conv_ln_g, conv_ln_b, w_out, ln1_g, ln1_b,
           peer_wq, peer_keys, peer_u, peer_v, ln2_g, ln2_b):
    bsz, seq, d = x.shape
    m = bsz * seq
    assert seq // CMP_STRIDE == N_CHUNK and d == D_MODEL
    x2 = x.reshape(m, d)
    xb = x2.astype(BF16)

    n_qkv = Q_COLS + KV_COLS
    w_qkv = w_in[:, :n_qkv].astype(BF16)
    w_gl = w_in[:, n_qkv:n_qkv + GATE_COLS]
    per_pair = 3 * HEADS_PER_PAIR
    w_gate = jnp.pad(w_gl.reshape(d, 2, per_pair), ((0, 0), (0, 0), (0, LANES - per_pair)))
    w_gate = w_gate.reshape(d, 2 * LANES).astype(BF16)
    w_conv = w_in[:, n_qkv + GATE_COLS:].astype(BF16)
    qkv = _matmul(xb, w_qkv, BF16, 1024, 512).reshape(bsz, seq, n_qkv)
    gates = _matmul(xb, w_gate, F32, 1024, 2 * LANES, act="sigmoid").reshape(bsz, seq, 2 * LANES)
    conv_ab = _matmul(xb, w_conv, F32, 1024, 512).reshape(bsz, seq, 2 * D_CONV)

    cmp_cols = 2 * N_KV * HEAD_DIM
    chunks = qkv[:, :, Q_COLS:Q_COLS + cmp_cols].reshape(bsz, N_CHUNK, CMP_STRIDE, 2 * N_KV, HEAD_DIM)
    chunks = jnp.transpose(chunks, (0, 3, 1, 2, 4)).reshape(bsz, 2 * N_KV, N_CHUNK, CMP_STRIDE * HEAD_DIM)
    half = CMP_STRIDE * HEAD_DIM
    pos = jnp.stack([cmp_pos_k.reshape(2, 1, half), cmp_pos_v.reshape(2, 1, half)])
    w1 = jnp.stack([cmp_w1_k.reshape(2, half, CMP_HIDDEN), cmp_w1_v.reshape(2, half, CMP_HIDDEN)]).astype(BF16)
    w2 = jnp.stack([cmp_w2_k, cmp_w2_v]).astype(BF16)
    kcv = _compress(chunks, pos, w1, w2)
    o_nsa = _nsa_attention2(qkv, kcv, gates)

    o_conv = _conv_group(conv_ab, dw_w, dw_b.reshape(1, -1), conv_ln_g.reshape(1, -1), conv_ln_b.reshape(1, -1))

    y, yt = _outproj_ln(o_nsa.reshape(m, Q_COLS), o_conv.reshape(m, D_CONV), x2, w_out.astype(BF16),
                        ln1_g.reshape(1, -1), ln1_b.reshape(1, -1))

    wqt = peer_wq.T.astype(BF16)
    keys = peer_keys.reshape(2 * PEER_HEADS, N_KEYS, -1).astype(BF16)
    s1, e1, s2, e2, tau = _peer_scores(yt, wqt, keys)
    ffn_t = _peer_mix(yt, peer_u.astype(BF16), peer_v.T.astype(BF16), s1, e1, s2, e2, tau)
    out = _final_ln(ffn_t, y, ln2_g.reshape(1, -1), ln2_b.reshape(1, -1))
    return out.reshape(bsz, seq, d)


def kernel(x, w_in, cmp_pos_k, cmp_w1_k, cmp_w2_k, cmp_pos_v, cmp_w1_v, cmp_w2_v, dw_w, dw_b,
           conv_ln_g, conv_ln_b, w_out, ln1_g, ln1_b, peer_wq, peer_keys, peer_u, peer_v, ln2_g, ln2_b):
    for l in range(DEPTH):
        x = _layer(x, w_in[l], cmp_pos_k[l], cmp_w1_k[l], cmp_w2_k[l], cmp_pos_v[l], cmp_w1_v[l],
                   cmp_w2_v[l], dw_w[l], dw_b[l], conv_ln_g[l], conv_ln_b[l], w_out[l], ln1_g[l],
                   ln1_b[l], peer_wq[l], peer_keys[l], peer_u[l], peer_v[l], ln2_g[l], ln2_b[l])
    return x
```
